```python
import math
import jax, jax.numpy as jnp
from jax import lax
import numpy as np

D_MODEL = 2048
BATCH = 2
SEQ = 4096
DEPTH = 1
DEC_BATCH = 32
DEC_SEQ = 4
PAST_LEN = 8192
PAGE_SIZE = 128

D_MIX = D_MODEL
D_ATTN = D_MIX // 2
D_RNN = D_MIX - D_ATTN
HEAD_DIM = 128
N_HEADS = D_ATTN // HEAD_DIM
IDX_HEADS = 8
IDX_DIM = 64
TOPK_MAX = 256
Q_BLOCK = 128
NUM_BUCKETS = 32
MAX_DISTANCE = 128
RNN_BLOCKS = 8
RNN_BW = D_RNN // RNN_BLOCKS
CONV_W = 4
LRU_C = 8.0
N_GROUPS = 4
EXP_PER_GROUP = 8
D_EXPERT = D_MODEL // 4
EPS = 1e-6

IN_SPLITS = (D_ATTN,
             2 * D_ATTN,
             3 * D_ATTN,
             3 * D_ATTN + IDX_HEADS * IDX_DIM,
             3 * D_ATTN + IDX_HEADS * IDX_DIM + IDX_DIM,
             3 * D_ATTN + IDX_HEADS * IDX_DIM + IDX_DIM + IDX_HEADS,
             3 * D_ATTN + IDX_HEADS * IDX_DIM + IDX_DIM + IDX_HEADS + D_RNN)
N_IN = IN_SPLITS[-1] + D_RNN

kernel_name = "hymba_rglru_dsa_hmoe_step"


def rms_norm(x, g):
    xf = x.astype(jnp.float32)
    y = xf * lax.rsqrt(jnp.mean(xf * xf, axis=-1, keepdims=True) + EPS)
    return (y * g.astype(jnp.float32)).astype(x.dtype)


def rel_bucket(dist):
    n = jnp.maximum(dist, 0)
    max_exact = NUM_BUCKETS // 2
    nf = jnp.maximum(n, 1).astype(jnp.float32)
    large = max_exact + (jnp.log(nf / max_exact) / math.log(MAX_DISTANCE / max_exact)
                         * (NUM_BUCKETS - max_exact)).astype(jnp.int32)
    large = jnp.minimum(large, NUM_BUCKETS - 1)
    return jnp.where(n < max_exact, n, large)


def gather_rows(rows, idx):
    return jax.vmap(lambda r, i: r[i])(rows, idx)


def index_topk(qi, wi, kidx, q_pos, top_k):
    n_keys = kidx.shape[1]
    dots = jnp.einsum("bthd,bld->bthl", qi, kidx).astype(jnp.float32) * IDX_DIM ** -0.5
    score = jnp.einsum("bth,bthl->btl", wi.astype(jnp.float32) * IDX_HEADS ** -0.5, jax.nn.relu(dots))
    key_pos = jnp.arange(n_keys, dtype=jnp.int32)
    causal = key_pos[None, :] <= q_pos[:, None]
    score = jnp.where(causal[None], score, -jnp.inf)
    _, idx = lax.top_k(score, top_k)
    valid = idx <= q_pos[None, :, None]
    return idx, valid


def attend_selected(q, kg, vg, idx, valid, q_pos, rel_bias):
    bias = rel_bias[rel_bucket(q_pos[None, :, None] - idx)].astype(jnp.float32)
    logits = jnp.einsum("bthd,btkhd->bthk", q, kg).astype(jnp.float32) * HEAD_DIM ** -0.5
    logits = logits + jnp.swapaxes(bias, 2, 3)
    logits = jnp.where(valid[:, :, None, :], logits, -1e30)
    p = jax.nn.softmax(logits, axis=-1).astype(vg.dtype)
    return jnp.einsum("bthk,btkhd->bthd", p, vg)


def prompt_attention(q, k, v, qi, ki, wi, rel_bias):
    B, T = q.shape[:2]
    top_k = min(TOPK_MAX, T // 4)
    nb = T // Q_BLOCK

    def blk(args):
        qb, qib, wib, start = args
        q_pos = start + jnp.arange(Q_BLOCK, dtype=jnp.int32)
        idx, valid = index_topk(qib, wib, ki, q_pos, top_k)
        return attend_selected(qb, gather_rows(k, idx), gather_rows(v, idx), idx, valid, q_pos, rel_bias)

    def to_blocks(a):
        return jnp.moveaxis(a.reshape(B, nb, Q_BLOCK, *a.shape[2:]), 1, 0)

    starts = jnp.arange(nb, dtype=jnp.int32) * Q_BLOCK
    out = lax.map(blk, (to_blocks(q), to_blocks(qi), to_blocks(wi), starts))
    return jnp.moveaxis(out, 0, 1).reshape(B, T, N_HEADS, HEAD_DIM)


def sample_attention(q, k, v, qi, ki, wi, pool_k, pool_v, pool_ki, page_table, rel_bias):
    Bd, T = q.shape[:2]
    past = page_table.shape[1] * PAGE_SIZE
    n_keys = past + T
    top_k = min(TOPK_MAX, n_keys // 4)
    ki_past = pool_ki[page_table].reshape(Bd, past, IDX_DIM)
    ki_all = jnp.concatenate([ki_past.astype(ki.dtype), ki], axis=1)
    q_pos = past + jnp.arange(T, dtype=jnp.int32)
    idx, valid = index_topk(qi, wi, ki_all, q_pos, top_k)
    in_past = idx < past
    pidx = jnp.minimum(idx, past - 1)
    phys_page = jnp.take_along_axis(page_table, (pidx // PAGE_SIZE).reshape(Bd, -1), axis=1)
    phys = phys_page.reshape(idx.shape) * PAGE_SIZE + pidx % PAGE_SIZE
    nidx = jnp.clip(idx - past, 0, T - 1)

    def pick(pool, new):
        flat = pool.reshape(-1, *pool.shape[2:])
        return jnp.where(in_past[..., None, None], flat[phys].astype(new.dtype), gather_rows(new, nidx))

    return attend_selected(q, pick(pool_k, k), pick(pool_v, v), idx, valid, q_pos, rel_bias)


def rglru_branch(x_r, g_r, conv_buf, h0, pos, conv_w, conv_b, w_rg, b_rg, w_ig, b_ig, lam):
    B, T, _ = x_r.shape
    xp = jnp.concatenate([conv_buf.astype(x_r.dtype), x_r], axis=1)
    xc = sum((conv_w[j] * xp[:, j:j + T] for j in range(CONV_W)), conv_b)
    new_buf = xp[:, -(CONV_W - 1):]
    xb = xc.reshape(B, T, RNN_BLOCKS, RNN_BW)
    r = jax.nn.sigmoid((jnp.einsum("btnc,ncd->btnd", xb, w_rg).reshape(B, T, D_RNN) + b_rg).astype(jnp.float32))
    i = jax.nn.sigmoid((jnp.einsum("btnc,ncd->btnd", xb, w_ig).reshape(B, T, D_RNN) + b_ig).astype(jnp.float32))
    log_a = -LRU_C * r * jax.nn.softplus(-lam.astype(jnp.float32))
    a = jnp.exp(log_a)
    mult = jnp.sqrt(-jnp.expm1(2.0 * log_a))
    mult = jnp.where((pos == 0)[None, :, None], 1.0, mult)
    b = mult * i * xc.astype(jnp.float32)
    b = b.at[:, 0].add(a[:, 0] * h0.astype(jnp.float32))

    def comb(lft, rgt):
        return (lft[0] * rgt[0], rgt[0] * lft[1] + rgt[1])

    _, h = lax.associative_scan(comb, (a, b), axis=1)
    y = h.astype(x_r.dtype) * jax.nn.gelu(g_r)
    return y, new_buf, h[:, -1].astype(h0.dtype)


def hier_moe(h, w_rg, b_rg, w_re, b_re, w_gate, w_up, w_down):
    B, T, D = h.shape
    xt = h.reshape(B * T, D)
    g_logits = (xt @ w_rg + b_rg).astype(jnp.float32)
    g_prob = jax.nn.softmax(g_logits, axis=-1)
    grp = jnp.argmax(g_logits, axis=-1)
    g_val = jnp.take_along_axis(g_prob, grp[:, None], axis=1)
    e_all = (jnp.einsum("nd,gde->nge", xt, w_re) + b_re).astype(jnp.float32)
    e_logits = jnp.take_along_axis(e_all, grp[:, None, None], axis=1)[:, 0]
    top_v, top_i = lax.top_k(e_logits, 2)
    top_w = jax.nn.softmax(top_v, axis=-1) * g_val
    e_w = jnp.sum(jax.nn.one_hot(top_i, EXP_PER_GROUP, dtype=jnp.float32) * top_w[..., None], axis=1)
    comb = (jax.nn.one_hot(grp, N_GROUPS, dtype=jnp.float32)[:, :, None] * e_w[:, None, :]).astype(xt.dtype)
    out = jnp.zeros_like(xt)
    for g in range(N_GROUPS):
        act = jax.nn.silu(jnp.einsum("nd,edf->nef", xt, w_gate[g])) * jnp.einsum("nd,edf->nef", xt, w_up[g])
        out = out + jnp.einsum("nef,efd->nd", act * comb[:, g, :, None], w_down[g])
    return out.reshape(B, T, D)


def trunk_layer(x, c, pos, conv_buf, h0, attn_fn, lp):
    B, T, _ = x.shape
    mod = jnp.einsum("bd,de->be", jax.nn.silu(c), lp["w_mod"]) + lp["b_mod"]
    sh1, sc1, gt1, sh2, sc2, gt2 = jnp.split(mod[:, None, :], 6, axis=-1)
    hn = rms_norm(x, lp["norm1"]) * (1 + sc1) + sh1
    proj = hn @ lp["w_in"]
    q, k, v, qi, ki, wi, xr, gr = jnp.split(proj, list(IN_SPLITS), axis=-1)
    q = q.reshape(B, T, N_HEADS, HEAD_DIM)
    k = k.reshape(B, T, N_HEADS, HEAD_DIM)
    v = v.reshape(B, T, N_HEADS, HEAD_DIM)
    qi = qi.reshape(B, T, IDX_HEADS, IDX_DIM)
    o_attn = attn_fn(q, k, v, qi, ki, wi).reshape(B, T, D_ATTN)
    y_rnn, new_buf, h_last = rglru_branch(xr, gr, conv_buf, h0, pos, lp["conv_w"], lp["conv_b"],
                                          lp["w_rgate"], lp["b_rgate"], lp["w_igate"], lp["b_igate"],
                                          lp["lru_lambda"])
    mixed = jnp.concatenate([rms_norm(o_attn, lp["g_attn"]), rms_norm(y_rnn, lp["g_rnn"])], axis=-1)
    x = x + gt1 * (mixed @ lp["w_out"])
    hn2 = rms_norm(x, lp["norm2"]) * (1 + sc2) + sh2
    x = x + gt2 * hier_moe(hn2, lp["w_rg"], lp["b_rg"], lp["w_re"], lp["b_re"],
                           lp["w_gate"], lp["w_up"], lp["w_down"])
    return x, (k, v, ki, new_buf, h_last)


def setup_inputs(seed: int = 0) -> dict:
    key = jax.random.key(seed)
    ks = iter(jax.random.split(key, 40))
    f32 = jnp.float32

    def nrm(shape, scale):
        return jax.random.normal(next(ks), shape, f32) * scale

    n_pages = PAST_LEN // PAGE_SIZE
    n_used = DEC_BATCH * n_pages
    n_pool = (n_used * 5) // 4
    perm = jax.random.permutation(next(ks), n_pool)
    page_table = perm[:n_used].reshape(DEC_BATCH, n_pages).astype(jnp.int32)

    a0 = jax.random.uniform(next(ks), (DEPTH, D_RNN), f32, minval=0.9, maxval=0.999)
    s0 = a0 ** (1.0 / LRU_C)
    lru_lambda = jnp.log(s0) - jnp.log1p(-s0)

    return {
        "x_prompt": nrm((BATCH, SEQ, D_MODEL), 1.0),
        "x_sample": nrm((DEC_BATCH, DEC_SEQ, D_MODEL), 1.0),
        "c_prompt": nrm((BATCH, D_MODEL), 1.0),
        "c_sample": nrm((DEC_BATCH, D_MODEL), 1.0),
        "cache_k": nrm((DEPTH, n_pool, PAGE_SIZE, N_HEADS, HEAD_DIM), 1.0),
        "cache_v": nrm((DEPTH, n_pool, PAGE_SIZE, N_HEADS, HEAD_DIM), 1.0),
        "cache_kidx": nrm((DEPTH, n_pool, PAGE_SIZE, IDX_DIM), 1.0),
        "state_conv": nrm((DEPTH, DEC_BATCH, CONV_W - 1, D_RNN), 1.0),
        "state_h": nrm((DEPTH, DEC_BATCH, D_RNN), 0.5),
        "page_table": page_table,
        "w_mod": nrm((DEPTH, D_MODEL, 6 * D_MODEL), 0.5 * D_MODEL ** -0.5),
        "b_mod": nrm((DEPTH, 6 * D_MODEL), 0.02),
        "norm1_g": 1.0 + nrm((DEPTH, D_MODEL), 0.02),
        "w_in": nrm((DEPTH, D_MODEL, N_IN), D_MODEL ** -0.5),
        "conv_w": nrm((DEPTH, CONV_W, D_RNN), CONV_W ** -0.5),
        "conv_b": nrm((DEPTH, D_RNN), 0.02),
        "w_rgate": nrm((DEPTH, RNN_BLOCKS, RNN_BW, RNN_BW), RNN_BW ** -0.5),
        "b_rgate": nrm((DEPTH, D_RNN), 0.02),
        "w_igate": nrm((DEPTH, RNN_BLOCKS, RNN_BW, RNN_BW), RNN_BW ** -0.5),
        "b_igate": nrm((DEPTH, D_RNN), 0.02),
        "lru_lambda": lru_lambda,
        "g_attn": 1.0 + nrm((DEPTH, D_ATTN), 0.02),
        "g_rnn": 1.0 + nrm((DEPTH, D_RNN), 0.02),
        "w_out": nrm((DEPTH, D_MIX, D_MODEL), D_MIX ** -0.5),
        "norm2_g": 1.0 + nrm((DEPTH, D_MODEL), 0.02),
        "w_router_group": nrm((DEPTH, D_MODEL, N_GROUPS), D_MODEL ** -0.5),
        "b_router_group": nrm((DEPTH, N_GROUPS), 0.01),
        "w_router_expert": nrm((DEPTH, N_GROUPS, D_MODEL, EXP_PER_GROUP), D_MODEL ** -0.5),
        "b_router_expert": nrm((DEPTH, N_GROUPS, EXP_PER_GROUP), 0.01),
        "w_gate": nrm((DEPTH, N_GROUPS, EXP_PER_GROUP, D_MODEL, D_EXPERT), D_MODEL ** -0.5),
        "w_up": nrm((DEPTH, N_GROUPS, EXP_PER_GROUP, D_MODEL, D_EXPERT), D_MODEL ** -0.5),
        "w_down": nrm((DEPTH, N_GROUPS, EXP_PER_GROUP, D_EXPERT, D_MODEL), D_EXPERT ** -0.5),
        "rel_bias": nrm((NUM_BUCKETS, N_HEADS), 0.5),
        "final_norm_g": 1.0 + nrm((D_MODEL,), 0.02),
    }


def reference(x_prompt, x_sample, c_prompt, c_sample, cache_k, cache_v, cache_kidx, state_conv, state_h,
              page_table, w_mod, b_mod, norm1_g, w_in, conv_w, conv_b, w_rgate, b_rgate, w_igate, b_igate,
              lru_lambda, g_attn, g_rnn, w_out, norm2_g, w_router_group, b_router_group, w_router_expert,
              b_router_expert, w_gate, w_up, w_down, rel_bias, final_norm_g):
    B, T = x_prompt.shape[:2]
    Bd, Td = x_sample.shape[:2]
    past = page_table.shape[1] * PAGE_SIZE
    pos_p = jnp.arange(T, dtype=jnp.int32)
    pos_s = past + jnp.arange(Td, dtype=jnp.int32)
    conv0 = jnp.zeros((B, CONV_W - 1, D_RNN), x_prompt.dtype)
    h0 = jnp.zeros((B, D_RNN), state_h.dtype)

    yp, ys = x_prompt, x_sample
    new_p, new_s = [], []
    for l in range(DEPTH):
        lp = {"w_mod": w_mod[l], "b_mod": b_mod[l], "norm1": norm1_g[l], "w_in": w_in[l],
              "conv_w": conv_w[l], "conv_b": conv_b[l], "w_rgate": w_rgate[l], "b_rgate": b_rgate[l],
              "w_igate": w_igate[l], "b_igate": b_igate[l], "lru_lambda": lru_lambda[l],
              "g_attn": g_attn[l], "g_rnn": g_rnn[l], "w_out": w_out[l], "norm2": norm2_g[l],
              "w_rg": w_router_group[l], "b_rg": b_router_group[l], "w_re": w_router_expert[l],
              "b_re": b_router_expert[l], "w_gate": w_gate[l], "w_up": w_up[l], "w_down": w_down[l]}

        def attn_p(q, k, v, qi, ki, wi):
            return prompt_attention(q, k, v, qi, ki, wi, rel_bias)

        def attn_s(q, k, v, qi, ki, wi):
            return sample_attention(q, k, v, qi, ki, wi, cache_k[l], cache_v[l], cache_kidx[l],
                                    page_table, rel_bias)

        yp, sp = trunk_layer(yp, c_prompt, pos_p, conv0, h0, attn_p, lp)
        ys, ss = trunk_layer(ys, c_sample, pos_s, state_conv[l], state_h[l], attn_s, lp)
        new_p.append(sp)
        new_s.append(ss)

    y_prompt = rms_norm(yp, final_norm_g)
    y_sample = rms_norm(ys, final_norm_g)
    k_prompt = jnp.stack([s[0] for s in new_p])
    v_prompt = jnp.stack([s[1] for s in new_p])
    kidx_prompt = jnp.stack([s[2] for s in new_p])
    conv_prompt = jnp.stack([s[3] for s in new_p])
    h_prompt = jnp.stack([s[4] for s in new_p])
    k_sample = jnp.stack([s[0] for s in new_s])
    v_sample = jnp.stack([s[1] for s in new_s])
    kidx_sample = jnp.stack([s[2] for s in new_s])
    conv_sample = jnp.stack([s[3] for s in new_s])
    h_sample = jnp.stack([s[4] for s in new_s])
    return (y_prompt, y_sample, k_prompt, v_prompt, kidx_prompt, conv_prompt, h_prompt,
            k_sample, v_sample, kidx_sample, conv_sample, h_sample)
```

```python
import functools
import math

import jax
import jax.numpy as jnp
from jax import lax
from jax.experimental import pallas as pl
from jax.experimental.pallas import tpu as pltpu

F32 = jnp.float32
BF16 = jnp.bfloat16
I32 = jnp.int32

EPS = 1e-6
TOPK_MAX = 256
NUM_BUCKETS = 32
MAX_DISTANCE = 128
LRU_C = 8.0
CONV_W = 4
NEG = -1e30
INT_MIN = -(2 ** 31)
LANES = 128
SUBLANES = 8
VMEM_LIMIT = 56 * 1024 * 1024

DN_NT = (((1,), (1,)), ((), ()))


def _cparams(*sem):
    return pltpu.CompilerParams(dimension_semantics=sem, vmem_limit_bytes=VMEM_LIMIT)


def _const_spec(shape):
    nd = len(shape)
    return pl.BlockSpec(shape, lambda *_: (0,) * nd, pipeline_mode=pl.Buffered(1))


def _rms(x, g):
    ms = jnp.mean(x * x, axis=-1, keepdims=True)
    return x * lax.rsqrt(ms + EPS) * g


def _norm_mod(x, g, sc, sh):
    return _rms(x, g) * (1.0 + sc) + sh


def _sortable(x):
    bits = lax.bitcast_convert_type(x, I32)
    return bits ^ ((bits >> 31) & 0x7FFFFFFF)


def _mod_kernel(c_ref, w_ref, b_ref, o_ref):
    c = c_ref[...]
    a = (c * jax.nn.sigmoid(c)).astype(BF16)
    o_ref[...] = jnp.dot(a, w_ref[...].astype(BF16), preferred_element_type=F32) + b_ref[...]


def _mod(c, w, b, tn=1024):
    r, d = c.shape
    n = w.shape[1]
    return pl.pallas_call(
        _mod_kernel,
        grid=(n // tn,),
        in_specs=[pl.BlockSpec((r, d), lambda j: (0, 0)),
                  pl.BlockSpec((d, tn), lambda j: (0, j)),
                  pl.BlockSpec((1, tn), lambda j: (0, j))],
        out_specs=pl.BlockSpec((r, tn), lambda j: (0, j)),
        out_shape=jax.ShapeDtypeStruct((r, n), F32),
        compiler_params=_cparams("arbitrary"),
        name="mod",
    )(c, w, b.reshape(1, n))


def _bias_kernel(rb_ref, dist_ref, o_ref):
    h = pl.program_id(0)
    n = jnp.maximum(dist_ref[...], 0)
    max_exact = NUM_BUCKETS // 2
    nf = jnp.maximum(n, 1).astype(F32)
    large = max_exact + (jnp.log(nf / max_exact) / math.log(MAX_DISTANCE / max_exact)
                         * (NUM_BUCKETS - max_exact)).astype(I32)
    large = jnp.minimum(large, NUM_BUCKETS - 1)
    bucket = jnp.where(n < max_exact, n, large)
    far = rb_ref[NUM_BUCKETS - 1, h]
    acc = jnp.zeros(bucket.shape, F32)
    for b in range(NUM_BUCKETS):
        acc = jnp.where(bucket == b, rb_ref[b, h] - far, acc)
    o_ref[...] = acc


def _bias_table(rel_bias, dist):
    nb, nh = rel_bias.shape
    r, c = dist.shape
    return pl.pallas_call(
        _bias_kernel,
        grid=(nh,),
        in_specs=[pl.BlockSpec(memory_space=pltpu.SMEM),
                  pl.BlockSpec((r, c), lambda h: (0, 0))],
        out_specs=pl.BlockSpec((None, r, c), lambda h: (h, 0, 0)),
        out_shape=jax.ShapeDtypeStruct((nh, r, c), F32),
        compiler_params=_cparams("arbitrary"),
        name="bias_table",
    )(rel_bias, dist)


def _proj_attn_kernel(x_ref, sc_ref, sh_ref, g_ref, w_ref,
                      q_ref, k_ref, v_ref, kb_ref, vb_ref, qi_ref, sm_ref, *, da, nqi, qscale):
    hn = _norm_mod(x_ref[...], g_ref[...], sc_ref[...], sh_ref[...]).astype(BF16)

    def mm(lo, hi):
        return jnp.dot(hn, w_ref[:, lo:hi], preferred_element_type=F32)

    q_ref[...] = (mm(0, da) * qscale).astype(BF16)
    k = mm(da, 2 * da)
    k_ref[...] = k
    kb_ref[...] = k.astype(BF16)
    v = mm(2 * da, 3 * da)
    v_ref[...] = v
    vb_ref[...] = v.astype(BF16)
    qi_ref[...] = mm(3 * da, 3 * da + nqi).astype(BF16)
    sm_ref[...] = mm(3 * da + nqi, 3 * da + nqi + LANES)


def _proj_attn(x, sc_spec, sh_spec, mod, g, w, tm, da, nqi, qscale):
    n, d = x.shape
    ncol = w.shape[1]
    row = lambda width: pl.BlockSpec((tm, width), lambda i: (i, 0))
    outs = [(da, BF16), (da, F32), (da, F32), (da, BF16), (da, BF16), (nqi, BF16), (LANES, F32)]
    return pl.pallas_call(
        functools.partial(_proj_attn_kernel, da=da, nqi=nqi, qscale=qscale),
        grid=(n // tm,),
        in_specs=[row(d), sc_spec, sh_spec, _const_spec((1, d)), _const_spec((d, ncol))],
        out_specs=[row(wd) for wd, _ in outs],
        out_shape=[jax.ShapeDtypeStruct((n, wd), dt) for wd, dt in outs],
        compiler_params=_cparams("arbitrary"),
        name="proj_attn",
    )(x, mod, mod, g, w)


def _gelu_tanh(x):
    return x * (0.5 * (1.0 + jnp.tanh(math.sqrt(2.0 / math.pi) * (x + 0.044715 * (x * x * x)))))


def _log1p(x):
    u = 1.0 + x
    return jnp.where(u == 1.0, x, jnp.log(u) * (x / (u - 1.0)))


def _expm1_nonpos(z):
    u = jnp.exp(z)
    near = jnp.where(u == 1.0, z, (u - 1.0) * (z / jnp.log(u)))
    return jnp.where(z < -1.0, u - 1.0, near)


def _rglru_coeffs(xc, wr_ref, br_ref, wi_ref, bi_ref, lam_ref, nb, bw):
    xcb = xc.astype(BF16)
    rp = [jnp.dot(xcb[:, n * bw:(n + 1) * bw], wr_ref[n], preferred_element_type=F32) for n in range(nb)]
    ip = [jnp.dot(xcb[:, n * bw:(n + 1) * bw], wi_ref[n], preferred_element_type=F32) for n in range(nb)]
    r = jax.nn.sigmoid(jnp.concatenate(rp, axis=1) + br_ref[...])
    i = jax.nn.sigmoid(jnp.concatenate(ip, axis=1) + bi_ref[...])
    lam = lam_ref[...]
    softplus_neg = jnp.maximum(-lam, 0.0) + _log1p(jnp.exp(-jnp.abs(lam)))
    log_a = (-LRU_C) * r * softplus_neg
    a = jnp.exp(log_a)
    mult = jnp.sqrt(-_expm1_nonpos(2.0 * log_a))
    return a, mult, i


def _rnn_kernel(x_ref, sc_ref, sh_ref, g_ref, w_ref, cw_ref, cb_ref, wr_ref, br_ref, wi_ref, bi_ref,
                lam_ref, gn_ref, y_ref, conv_ref, h_ref, xtail, hcar, *, dr, nb):
    t = pl.program_id(1)

    @pl.when(t == 0)
    def _():
        xtail[...] = jnp.zeros_like(xtail)
        hcar[...] = jnp.zeros_like(hcar)

    hn = _norm_mod(x_ref[...], g_ref[...], sc_ref[...], sh_ref[...]).astype(BF16)
    xr = jnp.dot(hn, w_ref[:, :dr], preferred_element_type=F32)
    gr = jnp.dot(hn, w_ref[:, dr:], preferred_element_type=F32)
    tm = xr.shape[0]
    row = lax.broadcasted_iota(I32, (tm, dr), 0)
    row8 = lax.broadcasted_iota(I32, (SUBLANES, dr), 0)

    tail = xtail[...]
    xc = cb_ref[...] + cw_ref[CONV_W - 1:CONV_W, :] * xr
    for j in range(1, CONV_W):
        sh = pltpu.roll(xr, j, axis=0)
        tl = pltpu.roll(tail, j, axis=0)
        top = jnp.where(row8 < j, tl, sh[:SUBLANES])
        shifted = jnp.concatenate([top, sh[SUBLANES:]], axis=0)
        xc = xc + cw_ref[CONV_W - 1 - j:CONV_W - j, :] * shifted
    xtail[...] = xr[tm - SUBLANES:]
    conv_ref[...] = xr[tm - SUBLANES:]

    a, mult, i = _rglru_coeffs(xc, wr_ref, br_ref, wi_ref, bi_ref, lam_ref, nb, dr // nb)
    mult = jnp.where((row == 0) & (t == 0), 1.0, mult)
    b = mult * i * xc

    d = 1
    while d < tm:
        keep = row >= d
        a_sh = jnp.where(keep, pltpu.roll(a, d, axis=0), 1.0)
        b_sh = jnp.where(keep, pltpu.roll(b, d, axis=0), 0.0)
        b = a * b_sh + b
        a = a * a_sh
        d *= 2
    h = b + a * hcar[...]
    hcar[...] = h[tm - 1:tm]
    h_ref[...] = h[tm - SUBLANES:]

    y = h * _gelu_tanh(gr)
    y_ref[...] = _rms(y, gn_ref[...]).astype(BF16)


def _rnn_prompt(x, mod3, g, w, cw, cb, wr, br, wi, bi, lam, gn, nbatch, tm, dr, nb):
    n, d = x.shape
    tps = n // nbatch // tm
    bw = dr // nb
    mod_spec = lambda chunk: pl.BlockSpec((None, 1, d), lambda b, t: (b, 0, chunk))
    vec = _const_spec((1, dr))
    return pl.pallas_call(
        functools.partial(_rnn_kernel, dr=dr, nb=nb),
        grid=(nbatch, tps),
        in_specs=[pl.BlockSpec((tm, d), lambda b, t: (b * tps + t, 0)), mod_spec(1), mod_spec(0),
                  _const_spec((1, d)), _const_spec((d, 2 * dr)), _const_spec((CONV_W, dr)), vec,
                  _const_spec((nb, bw, bw)), vec, _const_spec((nb, bw, bw)), vec, vec, vec],
        out_specs=[pl.BlockSpec((tm, dr), lambda b, t: (b * tps + t, 0)),
                   pl.BlockSpec((None, SUBLANES, dr), lambda b, t: (b, 0, 0)),
                   pl.BlockSpec((None, SUBLANES, dr), lambda b, t: (b, 0, 0))],
        out_shape=[jax.ShapeDtypeStruct((n, dr), BF16),
                   jax.ShapeDtypeStruct((nbatch, SUBLANES, dr), F32),
                   jax.ShapeDtypeStruct((nbatch, SUBLANES, dr), F32)],
        scratch_shapes=[pltpu.VMEM((SUBLANES, dr), F32), pltpu.VMEM((1, dr), F32)],
        compiler_params=_cparams("arbitrary", "arbitrary"),
        name="rnn_prompt",
    )(x, mod3, mod3, g, w, cw, cb, wr, br, wi, bi, lam, gn)


def _rnn_s_kernel(x_ref, sc_ref, sh_ref, g_ref, w_ref, cw_ref, cb_ref, wr_ref, br_ref, wi_ref, bi_ref,
                  lam_ref, gn_ref, cst_ref, h0_ref, y_ref, conv_ref, h_ref, *, dr, nb, nseq, nt):
    hn = _norm_mod(x_ref[...], g_ref[...], sc_ref[...], sh_ref[...]).astype(BF16)
    xr = jnp.dot(hn, w_ref[:, :dr], preferred_element_type=F32)
    gr = jnp.dot(hn, w_ref[:, dr:], preferred_element_type=F32)
    rows = nt * nseq
    xp = jnp.concatenate([cst_ref[...], xr], axis=0)
    xc = cb_ref[...]
    for j in range(CONV_W):
        xc = xc + cw_ref[j:j + 1, :] * xp[j * nseq:j * nseq + rows]
    conv_ref[...] = xp[rows:]

    a, mult, i = _rglru_coeffs(xc, wr_ref, br_ref, wi_ref, bi_ref, lam_ref, nb, dr // nb)
    b = mult * i * xc
    h = h0_ref[...]
    hs = []
    for t in range(nt):
        h = a[t * nseq:(t + 1) * nseq] * h + b[t * nseq:(t + 1) * nseq]
        hs.append(h)
    h_ref[...] = h
    y = jnp.concatenate(hs, axis=0) * _gelu_tanh(gr)
    y_ref[...] = _rms(y, gn_ref[...]).astype(BF16)


def _rnn_sample(x, sc, sh, g, w, cw, cb, wr, br, wi, bi, lam, gn, cst, h0, dr, nb, nseq, nt):
    rows, d = x.shape
    full = lambda a: _const_spec(a.shape)
    args = (x, sc, sh, g, w, cw, cb, wr, br, wi, bi, lam, gn, cst, h0)
    return pl.pallas_call(
        functools.partial(_rnn_s_kernel, dr=dr, nb=nb, nseq=nseq, nt=nt),
        grid=(1,),
        in_specs=[full(a) for a in args],
        out_specs=[pl.BlockSpec((rows, dr), lambda i: (0, 0)),
                   pl.BlockSpec(cst.shape, lambda i: (0, 0)),
                   pl.BlockSpec(h0.shape, lambda i: (0, 0))],
        out_shape=[jax.ShapeDtypeStruct((rows, dr), BF16),
                   jax.ShapeDtypeStruct(cst.shape, F32),
                   jax.ShapeDtypeStruct(h0.shape, F32)],
        compiler_params=_cparams("arbitrary"),
        name="rnn_sample",
    )(*args)


def _kth_largest_key(count_ge, shape, topk):
    def bit_body(i, tu):
        bit = jnp.left_shift(jnp.int32(1), 31 - i)
        cu = tu | bit
        cnt = count_ge(cu ^ INT_MIN)
        return jnp.where(cnt >= topk, cu, tu)

    tu = lax.fori_loop(0, 32, bit_body, jnp.zeros(shape, I32))
    return jnp.maximum(tu ^ INT_MIN, INT_MIN + 1)


def _tie_cut(count_eq_below, need, shape, nbits):
    def body(i, m):
        cm = m | jnp.left_shift(jnp.int32(1), nbits - 1 - i)
        return jnp.where(count_eq_below(cm) < need, cm, m)

    return lax.fori_loop(0, nbits, body, jnp.zeros(shape, I32))


def _select_mask(key, thr, idx, mcut):
    tie = jnp.where(idx <= mcut, 0.0, NEG)
    return jnp.where(key > thr, 0.0, jnp.where(key == thr, tie, NEG)).astype(F32)


def _pattn_kernel(q_ref, qi_ref, wt_ref, k_ref, vt_ref, ki_ref, bias_ref, g_ref, o_ref, skey, ot,
                  *, topk, nh, hd, nih, tq, nbits):
    qt = pl.program_id(1)
    tk = tq
    nk = qt + 1
    row = lax.broadcasted_iota(I32, (tk, tq), 0)
    col = lax.broadcasted_iota(I32, (tk, tq), 1)

    def koff(kt):
        return pl.multiple_of(kt * tk, tk)

    def score_body(kt, c):
        off = koff(kt)
        ki_t = ki_ref[pl.ds(off, tk), :]
        acc = jnp.zeros((tk, tq), F32)
        for h in range(nih):
            d = lax.dot_general(ki_t, qi_ref[h], DN_NT, preferred_element_type=F32)
            acc = acc + wt_ref[h:h + 1, :] * jnp.maximum(d, 0.0)
        valid = (off + row) <= (qt * tq + col)
        skey[pl.ds(off, tk), :] = jnp.where(valid, _sortable(acc), INT_MIN)
        return c

    lax.fori_loop(0, nk, score_body, 0)

    def count_ge(cand):
        def body(kt, c):
            blk = skey[pl.ds(koff(kt), tk), :]
            return c + jnp.sum((blk >= cand).astype(I32), axis=0, keepdims=True)
        return lax.fori_loop(0, nk, body, jnp.zeros((1, tq), I32))

    thr = _kth_largest_key(count_ge, (1, tq), topk)
    need = topk - count_ge(thr + 1)
    over = jnp.max(count_ge(thr)) > topk

    def count_eq_below(m):
        def body(kt, c):
            off = koff(kt)
            blk = skey[pl.ds(off, tk), :]
            hit = jnp.where(blk == thr, jnp.where((off + row) < m, 1, 0), 0)
            return c + jnp.sum(hit, axis=0, keepdims=True)
        return lax.fori_loop(0, nk, body, jnp.zeros((1, tq), I32))

    mcut = lax.cond(over,
                    lambda: _tie_cut(count_eq_below, need, (1, tq), nbits),
                    lambda: jnp.full((1, tq), 2 ** nbits, I32))

    def mask_body(kt, c):
        off = koff(kt)
        blk = skey[pl.ds(off, tk), :]
        skey[pl.ds(off, tk), :] = lax.bitcast_convert_type(_select_mask(blk, thr, off + row, mcut), I32)
        return c

    lax.fori_loop(0, nk, mask_body, 0)

    for h in range(nh):
        q_h = q_ref[:, h * hd:(h + 1) * hd]

        def tile(kt, carry, biased):
            m, l, acc = carry
            off = koff(kt)
            k_h = k_ref[pl.ds(off, tk), h * hd:(h + 1) * hd]
            s = lax.dot_general(k_h, q_h, DN_NT, preferred_element_type=F32)
            s = s + lax.bitcast_convert_type(skey[pl.ds(off, tk), :], F32)
            if biased:
                s = s + bias_ref[h, qt - kt]
            m_new = jnp.maximum(m, jnp.max(s, axis=0, keepdims=True))
            alpha = jnp.exp(m - m_new)
            p = jnp.exp(s - m_new)
            l = alpha * l + jnp.sum(p, axis=0, keepdims=True)
            v_h = vt_ref[kt, h * hd:(h + 1) * hd, :]
            acc = alpha * acc + jnp.dot(v_h, p.astype(BF16), preferred_element_type=F32)
            return m_new, l, acc

        carry = (jnp.full((1, tq), NEG, F32), jnp.zeros((1, tq), F32), jnp.zeros((hd, tq), F32))
        nfar = jnp.maximum(qt - 1, 0)
        carry = lax.fori_loop(0, nfar, functools.partial(tile, biased=False), carry)
        m, l, acc = lax.fori_loop(nfar, nk, functools.partial(tile, biased=True), carry)
        ot[h * hd:(h + 1) * hd, :] = acc * (1.0 / l)

    o = ot[...].T
    o_ref[...] = _rms(o, g_ref[...]).astype(BF16)


def _prompt_attention(q, qi_h, wt, kb, vt, kib, bias, g, nbatch, seq, tq, topk, nh, hd, nih):
    n, da = q.shape
    nq = seq // tq
    idim = kib.shape[1]
    nbits = int(seq).bit_length()
    return pl.pallas_call(
        functools.partial(_pattn_kernel, topk=topk, nh=nh, hd=hd, nih=nih, tq=tq, nbits=nbits),
        grid=(nbatch, nq),
        in_specs=[pl.BlockSpec((tq, da), lambda b, t: (b * nq + t, 0)),
                  pl.BlockSpec((nih, tq, idim), lambda b, t: (0, b * nq + t, 0)),
                  pl.BlockSpec((nih, tq), lambda b, t: (0, b * nq + t)),
                  pl.BlockSpec((seq, da), lambda b, t: (b, 0)),
                  pl.BlockSpec((None, nq, da, tq), lambda b, t: (b, 0, 0, 0)),
                  pl.BlockSpec((seq, idim), lambda b, t: (b, 0)),
                  _const_spec(bias.shape),
                  _const_spec((1, da))],
        out_specs=pl.BlockSpec((tq, da), lambda b, t: (b * nq + t, 0)),
        out_shape=jax.ShapeDtypeStruct((n, da), BF16),
        scratch_shapes=[pltpu.VMEM((seq, tq), I32), pltpu.VMEM((da, tq), F32)],
        compiler_params=_cparams("arbitrary", "arbitrary"),
        name="prompt_attention",
    )(q, qi_h, wt, kb, vt, kib, bias, g)


def _sidx_kernel(pt_ref, qi_ref, w_ref, kin_ref, *refs, topk, nih, npg, nt, past):
    page_refs = refs[:npg]
    madd_ref, maddn_ref, skey, snew = refs[npg:]
    c = pl.program_id(1)
    nc = pl.num_programs(1)
    ck = npg * page_refs[0].shape[0]

    def scores(keys):
        acc = jnp.zeros((SUBLANES, keys.shape[0]), F32)
        for h in range(nih):
            d = lax.dot_general(qi_ref[h], keys, DN_NT, preferred_element_type=F32)
            acc = acc + w_ref[h] * jnp.maximum(d, 0.0)
        return _sortable(acc)

    skey[c] = scores(jnp.concatenate([r[...] for r in page_refs], axis=0).astype(BF16))

    @pl.when(c == nc - 1)
    def _():
        nl = kin_ref.shape[0]
        lane_n = lax.broadcasted_iota(I32, (SUBLANES, nl), 1)
        row_n = lax.broadcasted_iota(I32, (SUBLANES, nl), 0)
        snew[...] = jnp.where((lane_n < nt) & (lane_n <= row_n), scores(kin_ref[...]), INT_MIN)
        lane = lax.broadcasted_iota(I32, (SUBLANES, ck), 1)
        nch = skey.shape[0]

        def count_ge(cand):
            cnt = jnp.sum((snew[...] >= cand).astype(I32), axis=1, keepdims=True)
            for cc in range(nch):
                cnt = cnt + jnp.sum((skey[cc] >= cand).astype(I32), axis=1, keepdims=True)
            return cnt

        thr = _kth_largest_key(count_ge, (SUBLANES, 1), topk)
        need = topk - count_ge(thr + 1)

        def count_eq_below(m):
            cnt = jnp.sum(jnp.where(snew[...] == thr, jnp.where((past + lane_n) < m, 1, 0), 0),
                          axis=1, keepdims=True)
            for cc in range(nch):
                hit = jnp.where(skey[cc] == thr, jnp.where((cc * ck + lane) < m, 1, 0), 0)
                cnt = cnt + jnp.sum(hit, axis=1, keepdims=True)
            return cnt

        nbits = int(past + nl).bit_length()
        mcut = _tie_cut(count_eq_below, need, (SUBLANES, 1), nbits)
        for cc in range(nch):
            madd_ref[cc] = _select_mask(skey[cc], thr, cc * ck + lane, mcut)
        maddn_ref[...] = _select_mask(snew[...], thr, past + lane_n, mcut)


def _sample_index(page_table, qi, w, kin, pool_ki, topk, nt, npg):
    nseq, nih, _, idim = qi.shape
    npages = page_table.shape[1]
    psz = pool_ki.shape[1]
    nc = npages // npg
    ck = npg * psz
    nl = kin.shape[1]
    past = npages * psz
    page_spec = lambda p: pl.BlockSpec((None, psz, idim), lambda b, c, pt: (pt[b, c * npg + p], 0, 0))
    grid_spec = pltpu.PrefetchScalarGridSpec(
        num_scalar_prefetch=1,
        grid=(nseq, nc),
        in_specs=[pl.BlockSpec((None, nih, SUBLANES, idim), lambda b, c, pt: (b, 0, 0, 0)),
                  pl.BlockSpec((None, nih, SUBLANES, 1), lambda b, c, pt: (b, 0, 0, 0)),
                  pl.BlockSpec((None, nl, idim), lambda b, c, pt: (b, 0, 0))]
                 + [page_spec(p) for p in range(npg)],
        out_specs=[pl.BlockSpec((None, nc, SUBLANES, ck), lambda b, c, pt: (b, 0, 0, 0)),
                   pl.BlockSpec((None, SUBLANES, nl), lambda b, c, pt: (b, 0, 0))],
        scratch_shapes=[pltpu.VMEM((nc, SUBLANES, ck), I32), pltpu.VMEM((SUBLANES, nl), I32)],
    )
    return pl.pallas_call(
        functools.partial(_sidx_kernel, topk=topk, nih=nih, npg=npg, nt=nt, past=past),
        grid_spec=grid_spec,
        out_shape=[jax.ShapeDtypeStruct((nseq, nc, SUBLANES, ck), F32),
                   jax.ShapeDtypeStruct((nseq, SUBLANES, nl), F32)],
        compiler_params=_cparams("arbitrary", "arbitrary"),
        name="sample_index",
    )(page_table, qi, w, kin, *([pool_ki] * npg))


def _sattn_kernel(pt_ref, q_ref, madd_ref, maddn_ref, bias_ref, biasn_ref, kn_ref, vn_ref, g_ref, *refs,
                  nh, hd, npg):
    k_refs = refs[:npg]
    v_refs = refs[npg:2 * npg]
    o_ref, m_s, l_s, acc_s = refs[2 * npg:]
    c = pl.program_id(1)
    nc = pl.num_programs(1)

    @pl.when(c == 0)
    def _():
        m_s[...] = jnp.full_like(m_s, NEG)
        l_s[...] = jnp.zeros_like(l_s)
        acc_s[...] = jnp.zeros_like(acc_s)

    def update(h, s, v_h):
        m = m_s[h]
        m_new = jnp.maximum(m, jnp.max(s, axis=1, keepdims=True))
        alpha = jnp.exp(m - m_new)
        p = jnp.exp(s - m_new)
        l_s[h] = alpha * l_s[h] + jnp.sum(p, axis=1, keepdims=True)
        acc_s[h] = alpha * acc_s[h] + jnp.dot(p.astype(BF16), v_h, preferred_element_type=F32)
        m_s[h] = m_new

    last = c == nc - 1
    madd = madd_ref[...]
    for h in range(nh):
        q_h = q_ref[:, h * hd:(h + 1) * hd]
        k_h = jnp.concatenate([r[:, h, :] for r in k_refs], axis=0).astype(BF16)
        v_h = jnp.concatenate([r[:, h, :] for r in v_refs], axis=0).astype(BF16)
        s = lax.dot_general(q_h, k_h, DN_NT, preferred_element_type=F32) + madd
        s = s + jnp.where(last, bias_ref[h], 0.0)
        update(h, s, v_h)

    @pl.when(last)
    def _():
        outs = []
        for h in range(nh):
            q_h = q_ref[:, h * hd:(h + 1) * hd]
            k_h = kn_ref[:, h * hd:(h + 1) * hd]
            s = lax.dot_general(q_h, k_h, DN_NT, preferred_element_type=F32) + maddn_ref[...] + biasn_ref[h]
            update(h, s, vn_ref[:, h * hd:(h + 1) * hd])
            outs.append(acc_s[h] * (1.0 / l_s[h]))
        o = jnp.concatenate(outs, axis=1)
        o_ref[...] = _rms(o, g_ref[...]).astype(BF16)


def _sample_attention(page_table, q, madd, maddn, bias, biasn, kn, vn, g, pool_k, pool_v, npg):
    nseq, _, da = q.shape
    _, psz, nh, hd = pool_k.shape
    nc = page_table.shape[1] // npg
    ck = npg * psz
    nl = kn.shape[1]
    page_spec = lambda p: pl.BlockSpec((None, psz, nh, hd), lambda b, c, pt: (pt[b, c * npg + p], 0, 0, 0))
    const = lambda a: pl.BlockSpec(a.shape, lambda b, c, pt: (0,) * a.ndim)
    grid_spec = pltpu.PrefetchScalarGridSpec(
        num_scalar_prefetch=1,
        grid=(nseq, nc),
        in_specs=[pl.BlockSpec((None, SUBLANES, da), lambda b, c, pt: (b, 0, 0)),
                  pl.BlockSpec((None, None, SUBLANES, ck), lambda b, c, pt: (b, c, 0, 0)),
                  pl.BlockSpec((None, SUBLANES, nl), lambda b, c, pt: (b, 0, 0)),
                  const(bias), const(biasn),
                  pl.BlockSpec((None, nl, da), lambda b, c, pt: (b, 0, 0)),
                  pl.BlockSpec((None, nl, da), lambda b, c, pt: (b, 0, 0)),
                  const(g)]
                 + [page_spec(p) for p in range(npg)] * 2,
        out_specs=pl.BlockSpec((None, SUBLANES, da), lambda b, c, pt: (b, 0, 0)),
        scratch_shapes=[pltpu.VMEM((nh, SUBLANES, 1), F32), pltpu.VMEM((nh, SUBLANES, 1), F32),
                        pltpu.VMEM((nh, SUBLANES, hd), F32)],
    )
    return pl.pallas_call(
        functools.partial(_sattn_kernel, nh=nh, hd=hd, npg=npg),
        grid_spec=grid_spec,
        out_shape=jax.ShapeDtypeStruct((nseq, SUBLANES, da), BF16),
        compiler_params=_cparams("arbitrary", "arbitrary"),
        name="sample_attention",
    )(page_table, q, madd, maddn, bias, biasn, kn, vn, g, *([pool_k] * npg), *([pool_v] * npg))


def _outproj_kernel(x_ref, a_ref, r_ref, gt_ref, sc_ref, sh_ref, g_ref, wo_ref, wr_ref, br_ref,
                    x2_ref, hn2_ref, rt_ref, *, da, ng, ne):
    mix = (jnp.dot(a_ref[...], wo_ref[:da, :], preferred_element_type=F32)
           + jnp.dot(r_ref[...], wo_ref[da:, :], preferred_element_type=F32))
    x2 = x_ref[...] + gt_ref[...] * mix
    x2_ref[...] = x2
    hn2 = _norm_mod(x2, g_ref[...], sc_ref[...], sh_ref[...])
    hn2_ref[...] = hn2
    logits = jnp.dot(hn2, wr_ref[...], preferred_element_type=F32,
                     precision=lax.Precision.HIGHEST) + br_ref[...]
    lane = lax.broadcasted_iota(I32, logits.shape, 1)
    ninf = -jnp.inf

    def first_max(v):
        mx = jnp.max(v, axis=1, keepdims=True)
        return mx, jnp.min(jnp.where(v == mx, lane, LANES - 1), axis=1, keepdims=True)

    gl = jnp.where(lane < ng, logits, ninf)
    gmax, grp = first_max(gl)
    gval = 1.0 / jnp.sum(jnp.exp(gl - gmax), axis=1, keepdims=True)
    lo = ng + grp * ne
    el = jnp.where((lane >= lo) & (lane < lo + ne), logits, ninf)
    v1, i1 = first_max(el)
    v2, i2 = first_max(jnp.where(lane == i1, ninf, el))
    e21 = jnp.exp(v2 - v1)
    w1 = gval / (1.0 + e21)
    w2 = w1 * e21
    rt = jnp.where(lane == 0, (i1 - ng).astype(F32),
                   jnp.where(lane == 1, (i2 - ng).astype(F32),
                             jnp.where(lane == 2, w1, jnp.where(lane == 3, w2, 0.0))))
    rt_ref[...] = rt


def _outproj(x, a, r, mod_specs, mod, g, wo, wr, br, tm, da, ng, ne):
    n, d = x.shape
    row = lambda width: pl.BlockSpec((tm, width), lambda i: (i, 0))
    return pl.pallas_call(
        functools.partial(_outproj_kernel, da=da, ng=ng, ne=ne),
        grid=(n // tm,),
        in_specs=[row(d), row(a.shape[1]), row(r.shape[1])] + list(mod_specs)
                 + [_const_spec((1, d)), _const_spec(wo.shape), _const_spec(wr.shape), _const_spec(br.shape)],
        out_specs=[row(d), row(d), row(LANES)],
        out_shape=[jax.ShapeDtypeStruct((n, d), F32), jax.ShapeDtypeStruct((n, d), F32),
                   jax.ShapeDtypeStruct((n, LANES), F32)],
        compiler_params=_cparams("arbitrary"),
        name="outproj_router",
    )(x, a, r, mod, mod, mod, g, wo, wr, br)


def _moe_kernel(tok_ref, te_ref, tf_ref, tv_ref, hn_ref, wg_ref, wu_ref, wd_ref, y_ref,
                xbuf, wgs, wus, wds, sem, *, tm):
    i = pl.program_id(0)
    n = pl.num_programs(0)

    def row_copy(tile, r, slot):
        tok = tok_ref[tile * tm + r]
        return pltpu.make_async_copy(hn_ref.at[pl.ds(tok, 1)], xbuf.at[slot, pl.ds(r, 1)], sem.at[slot])

    def issue(tile, slot):
        def body(r, c):
            row_copy(tile, r, slot).start()
            return c
        lax.fori_loop(0, tm, body, 0)

    @pl.when((i == 0) & (tv_ref[0] == 1))
    def _():
        issue(0, 0)

    nxt = jnp.minimum(i + 1, n - 1)

    @pl.when((i + 1 < n) & (tv_ref[nxt] == 1))
    def _():
        issue(i + 1, (i + 1) % 2)

    @pl.when(tf_ref[i] == 1)
    def _():
        wgs[...] = wg_ref[...].astype(BF16)
        wus[...] = wu_ref[...].astype(BF16)
        wds[...] = wd_ref[...].astype(BF16)

    @pl.when(tv_ref[i] == 1)
    def _():
        slot = i % 2

        def wbody(r, c):
            row_copy(i, r, slot).wait()
            return c
        lax.fori_loop(0, tm, wbody, 0)
        x = xbuf[slot].astype(BF16)
        gate = jnp.dot(x, wgs[...], preferred_element_type=F32)
        up = jnp.dot(x, wus[...], preferred_element_type=F32)
        act = (gate * jax.nn.sigmoid(gate) * up).astype(BF16)
        y_ref[...] = jnp.dot(act, wds[...], preferred_element_type=F32)

    @pl.when(tv_ref[i] == 0)
    def _():
        y_ref[...] = jnp.zeros_like(y_ref)


def _moe(slot_token, tile_e, tile_first, tile_valid, hn, wg, wu, wd, tm):
    nexp, d, f = wg.shape
    nt = tile_e.shape[0]
    grid_spec = pltpu.PrefetchScalarGridSpec(
        num_scalar_prefetch=4,
        grid=(nt,),
        in_specs=[pl.BlockSpec(memory_space=pl.ANY),
                  pl.BlockSpec((None, d, f), lambda i, tok, te, tf, tv: (te[i], 0, 0)),
                  pl.BlockSpec((None, d, f), lambda i, tok, te, tf, tv: (te[i], 0, 0)),
                  pl.BlockSpec((None, f, d), lambda i, tok, te, tf, tv: (te[i], 0, 0))],
        out_specs=pl.BlockSpec((tm, d), lambda i, tok, te, tf, tv: (i, 0)),
        scratch_shapes=[pltpu.VMEM((2, tm, d), F32), pltpu.VMEM((d, f), BF16), pltpu.VMEM((d, f), BF16),
                        pltpu.VMEM((f, d), BF16), pltpu.SemaphoreType.DMA((2,))],
    )
    return pl.pallas_call(
        functools.partial(_moe_kernel, tm=tm),
        grid_spec=grid_spec,
        out_shape=jax.ShapeDtypeStruct((nt * tm, d), F32),
        compiler_params=_cparams("arbitrary"),
        name="moe_experts",
    )(slot_token, tile_e, tile_first, tile_valid, hn, wg, wu, wd)


def _combine_kernel(pos_ref, x_ref, rt_ref, gt_ref, gf_ref, y_ref, o_ref, ybuf, sem, *, tm, row0):
    i = pl.program_id(0)
    base = row0 + i * tm

    def row_copy(r, j):
        p = pos_ref[2 * (base + r) + j]
        return pltpu.make_async_copy(y_ref.at[pl.ds(p, 1)], ybuf.at[j, pl.ds(r, 1)], sem.at[j])

    def issue(r, c):
        row_copy(r, 0).start()
        row_copy(r, 1).start()
        return c

    def wait(r, c):
        row_copy(r, 0).wait()
        row_copy(r, 1).wait()
        return c

    lax.fori_loop(0, tm, issue, 0)
    lax.fori_loop(0, tm, wait, 0)
    rt = rt_ref[...]
    moe = rt[:, 2:3] * ybuf[0] + rt[:, 3:4] * ybuf[1]
    x = x_ref[...] + gt_ref[...] * moe
    o_ref[...] = _rms(x, gf_ref[...])


def _combine(pos, x2, rt, gt_spec, mod, gf, y, tm, row0):
    n, d = x2.shape
    grid_spec = pltpu.PrefetchScalarGridSpec(
        num_scalar_prefetch=1,
        grid=(n // tm,),
        in_specs=[pl.BlockSpec((tm, d), lambda i, pos: (i, 0)),
                  pl.BlockSpec((tm, LANES), lambda i, pos: (i, 0)),
                  gt_spec,
                  pl.BlockSpec((1, d), lambda i, pos: (0, 0)),
                  pl.BlockSpec(memory_space=pl.ANY)],
        out_specs=pl.BlockSpec((tm, d), lambda i, pos: (i, 0)),
        scratch_shapes=[pltpu.VMEM((2, tm, d), F32), pltpu.SemaphoreType.DMA((2,))],
    )
    return pl.pallas_call(
        functools.partial(_combine_kernel, tm=tm, row0=row0),
        grid_spec=grid_spec,
        out_shape=jax.ShapeDtypeStruct((n, d), F32),
        compiler_params=_cparams("arbitrary"),
        name="combine_norm",
    )(pos, x2, rt, mod, gf, y)


def _dispatch_plan(route, nexp, tm):
    ntok = route.shape[0]
    eid = route[:, :2].astype(I32).reshape(-1)
    npair = eid.shape[0]
    onehot = (eid[:, None] == jnp.arange(nexp, dtype=I32)[None, :]).astype(I32)
    csum = jnp.cumsum(onehot, axis=0)
    rank = jnp.take_along_axis(csum, eid[:, None], axis=1)[:, 0] - 1
    cnt = csum[-1]
    padded = ((cnt + tm - 1) // tm) * tm
    ends = jnp.cumsum(padded)
    starts = ends - padded
    pos = starts[eid] + rank
    nt = (npair + nexp * (tm - 1) + tm - 1) // tm
    slot_token = jnp.zeros((nt * tm,), I32).at[pos].set(jnp.arange(npair, dtype=I32) // 2)
    tile_start = jnp.arange(nt, dtype=I32) * tm
    valid = tile_start < ends[-1]
    eff = jnp.minimum(tile_start, ends[-1] - tm)
    tile_e = jnp.minimum(jnp.sum((eff[:, None] >= ends[None, :]).astype(I32), axis=1), nexp - 1)
    first = jnp.concatenate([jnp.ones((1,), I32), (tile_e[1:] != tile_e[:-1]).astype(I32)])
    return pos, slot_token, tile_e, first, valid.astype(I32)


def kernel(x_prompt, x_sample, c_prompt, c_sample, cache_k, cache_v, cache_kidx, state_conv, state_h, page_table, w_mod, b_mod, norm1_g, w_in, conv_w, conv_b, w_rgate, b_rgate, w_igate, b_igate, lru_lambda, g_attn, g_rnn, w_out, norm2_g, w_router_group, b_router_group, w_router_expert, b_router_expert, w_gate, w_up, w_down, rel_bias, final_norm_g):
    nb_p, seq, d = x_prompt.shape
    nb_s, nt_s, _ = x_sample.shape
    depth = w_mod.shape[0]
    assert depth == 1, "single-layer step"
    l = 0
    _, _, psz, nh, hd = cache_k.shape
    da = nh * hd
    idim = cache_kidx.shape[-1]
    dr = state_h.shape[-1]
    n_in = w_in.shape[-1]
    nih = (n_in - 3 * da - idim - 2 * dr) // (idim + 1)
    nqi = nih * idim
    nrb = w_rgate.shape[1]
    ng, ne = b_router_expert.shape[1:]
    nexp = ng * ne
    npages = page_table.shape[1]
    past = npages * psz
    np_rows = nb_p * seq
    ns_rows = nb_s * nt_s
    tm = 256
    tq = 256
    npg = 8
    qscale = hd ** -0.5

    o_ki = 3 * da + nqi
    o_xr = o_ki + idim + nih
    w_l = w_in[l]
    w_a = jnp.concatenate([w_l[:, :o_ki], w_l[:, o_ki:o_xr],
                           jnp.zeros((d, LANES - idim - nih), F32)], axis=1).astype(BF16)
    w_r = w_l[:, o_xr:].astype(BF16)
    wrg = w_rgate[l].astype(BF16)
    wig = w_igate[l].astype(BF16)
    wo = w_out[l].astype(BF16)
    w_router = jnp.concatenate(
        [w_router_group[l], jnp.transpose(w_router_expert[l], (1, 0, 2)).reshape(d, nexp),
         jnp.zeros((d, LANES - ng - nexp), F32)], axis=1)
    b_router = jnp.concatenate([b_router_group[l], b_router_expert[l].reshape(-1),
                                jnp.zeros((LANES - ng - nexp,), F32)]).reshape(1, LANES)
    vec = lambda a: a.reshape(1, -1)

    nseq = nb_p + nb_s
    nseq_pad = -(-nseq // SUBLANES) * SUBLANES
    c_all = jnp.concatenate([c_prompt, c_sample, jnp.zeros((nseq_pad - nseq, d), F32)], axis=0)
    mod = _mod(c_all, w_mod[l], b_mod[l])
    mod_p = mod[:nb_p].reshape(nb_p, 1, 6 * d)
    mod_s = mod[nb_p:nseq]
    mod_s_bm = jnp.repeat(mod_s, nt_s, axis=0)
    mod_s_tm = jnp.tile(mod_s, (nt_s, 1))
    tps = seq // tm
    pspec = lambda chunk: pl.BlockSpec((None, 1, d), lambda i: (i // tps, 0, chunk))
    sspec = lambda chunk: pl.BlockSpec((ns_rows, d), lambda i: (0, chunk))

    xp = x_prompt.reshape(np_rows, d)
    xs = x_sample.reshape(ns_rows, d)
    g1 = vec(norm1_g[l])

    q_p, k_p, v_p, kb_p, vb_p, qi_p, sm_p = _proj_attn(xp, pspec(1), pspec(0), mod_p, g1, w_a, tm, da, nqi, qscale)
    q_s, k_s, v_s, kb_s, vb_s, qi_s, sm_s = _proj_attn(xs, sspec(1), sspec(0), mod_s_bm, g1, w_a, ns_rows, da, nqi, qscale)

    rnn_w = (g1, w_r, conv_w[l], vec(conv_b[l]), wrg, vec(b_rgate[l]), wig, vec(b_igate[l]),
             vec(lru_lambda[l]), vec(g_rnn[l]))
    yr_p, conv_p, h_p = _rnn_prompt(xp, mod_p, *rnn_w, nbatch=nb_p, tm=tm, dr=dr, nb=nrb)
    xs_tm = jnp.transpose(x_sample, (1, 0, 2)).reshape(ns_rows, d)
    cst = jnp.transpose(state_conv[l], (1, 0, 2)).reshape((CONV_W - 1) * nb_s, dr)
    yr_s_tm, conv_s, h_s = _rnn_sample(xs_tm, mod_s_tm[:, d:2 * d], mod_s_tm[:, :d], *rnn_w, cst, state_h[l],
                                       dr=dr, nb=nrb, nseq=nb_s, nt=nt_s)
    yr_s = jnp.transpose(yr_s_tm.reshape(nt_s, nb_s, dr), (1, 0, 2)).reshape(ns_rows, dr)

    ii = jnp.arange(tq, dtype=I32)
    dist0 = ii[None, :] - ii[:, None]
    bias_p = _bias_table(rel_bias, jnp.concatenate([dist0, dist0 + tq], axis=0)).reshape(nh, 2, tq, tq)
    nq = seq // tq
    qi_ph = jnp.transpose(qi_p.reshape(np_rows, nih, idim), (1, 0, 2))
    wt_p = jnp.transpose(sm_p[:, idim:idim + nih])
    vt_p = jnp.transpose(vb_p.reshape(nb_p, nq, tq, da), (0, 1, 3, 2))
    kib_p = sm_p[:, :idim].astype(BF16)
    top_p = min(TOPK_MAX, seq // 4)
    oa_p = _prompt_attention(q_p, qi_ph, wt_p, kb_p, vt_p, kib_p, bias_p, vec(g_attn[l]),
                             nbatch=nb_p, seq=seq, tq=tq, topk=top_p, nh=nh, hd=hd, nih=nih)

    top_s = min(TOPK_MAX, (past + nt_s) // 4)
    pad_rows = lambda a, rows: jnp.pad(a, ((0, 0), (0, rows - a.shape[1])) + ((0, 0),) * (a.ndim - 2))
    qi_s4 = pad_rows(jnp.transpose(qi_s.reshape(nb_s, nt_s, nih, idim), (0, 2, 1, 3)).reshape(nb_s * nih, nt_s, idim),
                     SUBLANES).reshape(nb_s, nih, SUBLANES, idim)
    w_s4 = pad_rows(jnp.transpose(sm_s[:, idim:idim + nih].reshape(nb_s, nt_s, nih), (0, 2, 1))
                    .reshape(nb_s * nih, nt_s), SUBLANES).reshape(nb_s, nih, SUBLANES, 1)
    kin_s = pad_rows(sm_s[:, :idim].astype(BF16).reshape(nb_s, nt_s, idim), LANES)
    madd, maddn = _sample_index(page_table, qi_s4, w_s4, kin_s, cache_kidx[l], top_s, nt_s, npg)
    ck = npg * psz
    jj = jnp.arange(SUBLANES, dtype=I32)[:, None]
    bias_s = _bias_table(rel_bias, ck + jj - jnp.arange(ck, dtype=I32)[None, :])
    bias_n = _bias_table(rel_bias, jj - jnp.arange(LANES, dtype=I32)[None, :])
    q_s3 = pad_rows(q_s.reshape(nb_s, nt_s, da), SUBLANES)
    kn_s = pad_rows(kb_s.reshape(nb_s, nt_s, da), LANES)
    vn_s = pad_rows(vb_s.reshape(nb_s, nt_s, da), LANES)
    oa_s = _sample_attention(page_table, q_s3, madd, maddn, bias_s, bias_n, kn_s, vn_s, vec(g_attn[l]),
                             cache_k[l], cache_v[l], npg)
    oa_s = oa_s[:, :nt_s].reshape(ns_rows, da)

    g2 = vec(norm2_g[l])
    x2_p, hn2_p, rt_p = _outproj(xp, oa_p, yr_p, [pspec(2), pspec(4), pspec(3)], mod_p, g2, wo, w_router,
                                 b_router, tm, da, ng, ne)
    x2_s, hn2_s, rt_s = _outproj(xs, oa_s, yr_s, [sspec(2), sspec(4), sspec(3)], mod_s_bm, g2, wo, w_router,
                                 b_router, ns_rows, da, ng, ne)

    hn2 = jnp.concatenate([hn2_p, hn2_s], axis=0)
    route = jnp.concatenate([rt_p, rt_s], axis=0)
    pos, slot_token, tile_e, tile_first, tile_valid = _dispatch_plan(route, nexp, tm)
    f = w_gate.shape[-1]
    y_sorted = _moe(slot_token, tile_e, tile_first, tile_valid, hn2,
                    w_gate[l].reshape(nexp, d, f), w_up[l].reshape(nexp, d, f), w_down[l].reshape(nexp, f, d), tm)
    gf = vec(final_norm_g)
    cps = lambda chunk: pl.BlockSpec((None, 1, d), lambda i, pos: (i // tps, 0, chunk))
    css = lambda chunk: pl.BlockSpec((ns_rows, d), lambda i, pos: (0, chunk))
    y_p = _combine(pos, x2_p, rt_p, cps(5), mod_p, gf, y_sorted, tm, 0)
    y_s = _combine(pos, x2_s, rt_s, css(5), mod_s_bm, gf, y_sorted, ns_rows, np_rows)

    lead = lambda a: a[None]
    conv_s_out = jnp.transpose(conv_s.reshape(CONV_W - 1, nb_s, dr), (1, 0, 2))
    return (y_p.reshape(nb_p, seq, d), y_s.reshape(nb_s, nt_s, d),
            lead(k_p.reshape(nb_p, seq, nh, hd)), lead(v_p.reshape(nb_p, seq, nh, hd)),
            lead(sm_p[:, :idim].reshape(nb_p, seq, idim)),
            lead(conv_p[:, SUBLANES - (CONV_W - 1):]), lead(h_p[:, SUBLANES - 1]),
            lead(k_s.reshape(nb_s, nt_s, nh, hd)), lead(v_s.reshape(nb_s, nt_s, nh, hd)),
            lead(sm_s[:, :idim].reshape(nb_s, nt_s, idim)),
            lead(conv_s_out), lead(h_s))
```
